```python
import jax, jax.numpy as jnp
from jax import lax
import numpy as np

D_MODEL = 2048
BATCH = 1
SEQ = 8192
DEPTH = 4

GRID_W = 64
CTX_LEN = 256
N_MIXERS = 2
LRU_WIDTH = D_MODEL
LRU_HEADS = 8
LRU_HEAD_DIM = LRU_WIDTH // LRU_HEADS
CONV_WIDTH = 4
LRU_C = 8.0
CHUNK = 128
SGU_WIDTH = D_MODEL
SGU_GROUPS = 8
SGU_GROUP_DIM = SGU_WIDTH // SGU_GROUPS
ROWS_PER_CHUNK = CHUNK // GRID_W
FFN_HIDDEN = ((8 * D_MODEL // 3 + 255) // 256) * 256
N_EXPERTS = 8
TOP_K = 2
MOE_BLOCK = 128
N_MOD = 6
EPS = 1e-6
N_EVEN = (DEPTH + 1) // 2
N_ODD = DEPTH // 2

kernel_name = "hybrid_rglru_chunkmlp_moe_dit"


def rmsnorm(x, g):
    xf = x.astype(jnp.float32)
    y = xf * lax.rsqrt(jnp.mean(xf * xf, axis=-1, keepdims=True) + EPS)
    return (y * g.astype(jnp.float32)).astype(x.dtype)


def layernorm(x, g, b):
    xf = x.astype(jnp.float32)
    mu = jnp.mean(xf, axis=-1, keepdims=True)
    xc = xf - mu
    y = xc * lax.rsqrt(jnp.mean(xc * xc, axis=-1, keepdims=True) + EPS)
    return (y * g.astype(jnp.float32) + b.astype(jnp.float32)).astype(x.dtype)


def modulate(h, shift, scale):
    return h * (1 + scale[:, None, :]) + shift[:, None, :]


def swiglu(t, w1, w3, w2):
    return (jax.nn.silu(t @ w1) * (t @ w3)) @ w2


def _dwconv_centred(x, w, b):
    n = x.shape[1]
    pad_l = CONV_WIDTH // 2
    pad_r = CONV_WIDTH - 1 - pad_l
    xp = jnp.pad(x, ((0, 0), (pad_l, pad_r), (0, 0)))
    y = b
    for k in range(CONV_WIDTH):
        y = y + w[k] * xp[:, k:k + n]
    return y


def _rglru_coeffs(xc, w_a, b_a, w_x, b_x, lam):
    bsz, n, _ = xc.shape
    xh = xc.reshape(bsz, n, LRU_HEADS, LRU_HEAD_DIM)
    gate_r = jax.nn.sigmoid((jnp.einsum("bnhi,hij->bnhj", xh, w_a) + b_a).astype(jnp.float32))
    gate_i = jax.nn.sigmoid((jnp.einsum("bnhi,hij->bnhj", xh, w_x) + b_x).astype(jnp.float32))
    log_a = -LRU_C * gate_r * jax.nn.softplus(-lam.astype(jnp.float32)).reshape(LRU_HEADS, LRU_HEAD_DIM)
    a = jnp.exp(log_a)
    b = jnp.sqrt(-jnp.expm1(2.0 * log_a)) * gate_i * xh.astype(jnp.float32)
    return a.reshape(bsz, n, LRU_WIDTH), b.reshape(bsz, n, LRU_WIDTH)


def _linear_scan(a, b, h0, reverse):
    edge = -1 if reverse else 0
    b = b.at[:, edge].add(a[:, edge] * h0)

    def combine(left, right):
        a_l, b_l = left
        a_r, b_r = right
        return a_l * a_r, a_r * b_l + b_r

    _, h = lax.associative_scan(combine, (a, b), axis=1, reverse=reverse)
    return h


def rglru_mixer(h_lat, h_ctx, w_in, conv_w, conv_b, w_a, b_a, w_x, b_x, lam, w_out, ctx_out):
    def branches(h):
        z = h @ w_in
        xb, gb = z[..., :LRU_WIDTH], z[..., LRU_WIDTH:]
        return _dwconv_centred(xb, conv_w, conv_b), gb

    xc_c, g_c = branches(h_ctx)
    xc_l, g_l = branches(h_lat)
    ys_l, ys_c = [], []
    for d, reverse in enumerate((False, True)):
        a_c, b_c = _rglru_coeffs(xc_c, w_a[d], b_a[d], w_x[d], b_x[d], lam[d])
        hc = _linear_scan(a_c, b_c, jnp.zeros_like(a_c[:, 0]), reverse)
        h_final = hc[:, 0] if reverse else hc[:, -1]
        a_l, b_l = _rglru_coeffs(xc_l, w_a[d], b_a[d], w_x[d], b_x[d], lam[d])
        ys_l.append(_linear_scan(a_l, b_l, h_final, reverse))
        if ctx_out:
            ys_c.append(hc)
    y_l = (ys_l[0] + ys_l[1]).astype(h_lat.dtype)
    out_l = (y_l * jax.nn.gelu(g_l)) @ w_out
    out_c = None
    if ctx_out:
        y_c = (ys_c[0] + ys_c[1]).astype(h_ctx.dtype)
        out_c = (y_c * jax.nn.gelu(g_c)) @ w_out
    return out_l, out_c


def chunk_mlp_mixer(h, n_chunks, w_in, ln_g, ln_b, w_s, b_s, w_out):
    bsz, n, _ = h.shape
    z = jax.nn.gelu(h @ w_in)
    u, v = z[..., :SGU_WIDTH], z[..., SGU_WIDTH:]
    v = layernorm(v, ln_g, ln_b)
    vg = v.reshape(bsz, n_chunks, CHUNK, SGU_GROUPS, SGU_GROUP_DIM)
    vm = jnp.einsum("gpq,bnqgc->bnpgc", w_s, vg) + b_s.T[None, None, :, :, None]
    return (u * vm.reshape(bsz, n, SGU_WIDTH)) @ w_out


def moe_swiglu(t, w_router, w1, w3, w2):
    n, d = t.shape
    nk = n * TOP_K
    logits = (t @ w_router).astype(jnp.float32)
    top_vals, top_idx = lax.top_k(logits, TOP_K)
    gates = jax.nn.softmax(top_vals, axis=-1)
    expert = top_idx.reshape(nk)
    token = jnp.repeat(jnp.arange(n), TOP_K)
    gate = gates.reshape(nk)
    order = jnp.argsort(expert)
    e_sorted, tok_sorted, gate_sorted = expert[order], token[order], gate[order]
    counts = jnp.bincount(expert, length=N_EXPERTS)
    starts = jnp.cumsum(counts) - counts
    padded = ((counts + MOE_BLOCK - 1) // MOE_BLOCK) * MOE_BLOCK
    pends = jnp.cumsum(padded)
    pstarts = pends - padded
    pos = pstarts[e_sorted] + (jnp.arange(nk) - starts[e_sorted])
    n_blocks = -(-(nk + N_EXPERTS * (MOE_BLOCK - 1)) // MOE_BLOCK)
    cap = n_blocks * MOE_BLOCK
    xbuf = jnp.zeros((cap, d), t.dtype).at[pos].set(t[tok_sorted])
    block_expert = jnp.clip(jnp.searchsorted(pends, jnp.arange(n_blocks) * MOE_BLOCK, side="right"), 0, N_EXPERTS - 1)

    def expert_block(args):
        xb, e = args
        return swiglu(xb, w1[e], w3[e], w2[e])

    ybuf = lax.map(expert_block, (xbuf.reshape(n_blocks, MOE_BLOCK, d), block_expert))
    y_sorted = ybuf.reshape(cap, d)[pos]
    return jnp.zeros_like(t).at[tok_sorted].add(y_sorted * gate_sorted[:, None].astype(t.dtype))


def setup_inputs(seed: int = 0) -> dict:
    key = jax.random.key(seed)
    ks = jax.random.split(key, 32)
    f32 = jnp.float32
    D, W, E, F = D_MODEL, LRU_WIDTH, SGU_WIDTH, FFN_HIDDEN

    def nrm(k, shape, scale):
        return jax.random.normal(k, shape, f32) * scale

    lam_u = jax.random.uniform(ks[12], (N_EVEN, 2, W), f32, minval=0.9, maxval=0.999)
    return {
        "x": nrm(ks[0], (BATCH, SEQ, D), 1.0),
        "c": nrm(ks[1], (BATCH, D), 1.0),
        "ctx": nrm(ks[2], (BATCH, CTX_LEN, D), 1.0),
        "c_ctx": nrm(ks[3], (D,), 1.0),
        "norm_g": 1.0 + nrm(ks[4], (DEPTH, 2, D), 0.1),
        "final_g": 1.0 + nrm(ks[5], (D,), 0.1),
        "w_mod": nrm(ks[6], (DEPTH, D, N_MOD * D), D ** -0.5),
        "b_mod": nrm(ks[7], (DEPTH, N_MOD * D), 0.01),
        "lru_w_in": nrm(ks[8], (N_EVEN, D, 2 * W), D ** -0.5),
        "lru_conv_w": nrm(ks[9], (N_EVEN, CONV_WIDTH, W), CONV_WIDTH ** -0.5),
        "lru_conv_b": nrm(ks[10], (N_EVEN, W), 0.01),
        "lru_w_a": nrm(ks[11], (N_EVEN, 2, LRU_HEADS, LRU_HEAD_DIM, LRU_HEAD_DIM), LRU_HEAD_DIM ** -0.5),
        "lru_b_a": nrm(ks[13], (N_EVEN, 2, LRU_HEADS, LRU_HEAD_DIM), 0.01),
        "lru_w_x": nrm(ks[14], (N_EVEN, 2, LRU_HEADS, LRU_HEAD_DIM, LRU_HEAD_DIM), LRU_HEAD_DIM ** -0.5),
        "lru_b_x": nrm(ks[15], (N_EVEN, 2, LRU_HEADS, LRU_HEAD_DIM), 0.01),
        "lru_lambda": jnp.log(lam_u) - jnp.log1p(-lam_u),
        "lru_w_out": nrm(ks[16], (N_EVEN, W, D), W ** -0.5),
        "sgu_w_in": nrm(ks[17], (N_ODD, D, 2 * E), D ** -0.5),
        "sgu_ln_g": 1.0 + nrm(ks[18], (N_ODD, E), 0.1),
        "sgu_ln_b": nrm(ks[19], (N_ODD, E), 0.01),
        "sgu_w_s": nrm(ks[20], (N_ODD, SGU_GROUPS, CHUNK, CHUNK), CHUNK ** -0.5),
        "sgu_b_s": 1.0 + nrm(ks[21], (N_ODD, SGU_GROUPS, CHUNK), 0.1),
        "sgu_w_out": nrm(ks[22], (N_ODD, E, D), E ** -0.5),
        "ffn_w1": nrm(ks[23], (N_EVEN, D, F), D ** -0.5),
        "ffn_w3": nrm(ks[24], (N_EVEN, D, F), D ** -0.5),
        "ffn_w2": nrm(ks[25], (N_EVEN, F, D), F ** -0.5),
        "moe_router": nrm(ks[26], (N_ODD, D, N_EXPERTS), D ** -0.5),
        "moe_w1": nrm(ks[27], (N_ODD, N_EXPERTS, D, F), D ** -0.5),
        "moe_w3": nrm(ks[28], (N_ODD, N_EXPERTS, D, F), D ** -0.5),
        "moe_w2": nrm(ks[29], (N_ODD, N_EXPERTS, F, D), F ** -0.5),
    }


def reference(x, c, ctx, c_ctx, norm_g, final_g, w_mod, b_mod,
              lru_w_in, lru_conv_w, lru_conv_b, lru_w_a, lru_b_a, lru_w_x, lru_b_x, lru_lambda, lru_w_out,
              sgu_w_in, sgu_ln_g, sgu_ln_b, sgu_w_s, sgu_b_s, sgu_w_out,
              ffn_w1, ffn_w3, ffn_w2,
              moe_router, moe_w1, moe_w3, moe_w2):
    bsz, n_lat, d = x.shape
    n_ctx = ctx.shape[1]
    rows = n_lat // GRID_W
    n_lat_chunks = rows // ROWS_PER_CHUNK
    n_ctx_chunks = n_ctx // CHUNK
    sc = jax.nn.silu(c)
    scc = jax.nn.silu(c_ctx)[None]

    for i in range(DEPTH):
        j = i // N_MIXERS
        is_a = (i % N_MIXERS) == 0
        ctx_next = any(k % N_MIXERS == 0 for k in range(i + 1, DEPTH))
        ctx_read = is_a or ctx_next
        mod_l = (sc @ w_mod[i] + b_mod[i]).reshape(bsz, N_MOD, d)
        mod_c = (scc @ w_mod[i] + b_mod[i]).reshape(1, N_MOD, d)

        hl = modulate(rmsnorm(x, norm_g[i, 0]), mod_l[:, 0], mod_l[:, 1])
        hc = modulate(rmsnorm(ctx, norm_g[i, 0]), mod_c[:, 0], mod_c[:, 1]) if ctx_read else None
        if is_a:
            ml, mc = rglru_mixer(hl, hc, lru_w_in[j], lru_conv_w[j], lru_conv_b[j],
                                 lru_w_a[j], lru_b_a[j], lru_w_x[j], lru_b_x[j], lru_lambda[j],
                                 lru_w_out[j], ctx_next)
        else:
            ml = chunk_mlp_mixer(hl, n_lat_chunks, sgu_w_in[j], sgu_ln_g[j], sgu_ln_b[j],
                                 sgu_w_s[j], sgu_b_s[j], sgu_w_out[j])
            mc = (chunk_mlp_mixer(hc, n_ctx_chunks, sgu_w_in[j], sgu_ln_g[j], sgu_ln_b[j],
                                  sgu_w_s[j], sgu_b_s[j], sgu_w_out[j]) if ctx_next else None)
        x = x + mod_l[:, 2][:, None, :] * ml
        if ctx_next:
            ctx = ctx + mod_c[:, 2][:, None, :] * mc

        fl = modulate(rmsnorm(x, norm_g[i, 1]), mod_l[:, 3], mod_l[:, 4])
        if ctx_next:
            fc = modulate(rmsnorm(ctx, norm_g[i, 1]), mod_c[:, 3], mod_c[:, 4])
            tokens = jnp.concatenate([fc, fl], axis=1)
        else:
            tokens = fl
        n_tok = tokens.shape[1]
        flat = tokens.reshape(bsz * n_tok, d)
        if i % 2 == 0:
            y = swiglu(flat, ffn_w1[j], ffn_w3[j], ffn_w2[j])
        else:
            y = moe_swiglu(flat, moe_router[j], moe_w1[j], moe_w3[j], moe_w2[j])
        y = y.reshape(bsz, n_tok, d)
        if ctx_next:
            ctx = ctx + mod_c[:, 5][:, None, :] * y[:, :n_ctx]
            x = x + mod_l[:, 5][:, None, :] * y[:, n_ctx:]
        else:
            x = x + mod_l[:, 5][:, None, :] * y

    return rmsnorm(x, final_g)
```

```python
import functools

import jax
import jax.numpy as jnp
from jax import lax
from jax.experimental import pallas as pl
from jax.experimental.pallas import tpu as pltpu

D = 2048
SEQ = 8192
CTX = 256
DEPTH = 4
W = D
HEADS = 8
HD = W // HEADS
CONV = 4
LRU_C = 8.0
CHUNK = 128
GROUPS = 8
GD = W // GROUPS
F = 5632
E = 8
NMOD = 6
EPS = 1e-6

LANES = 128
SUBLANES = 8
VMEM_LIMIT = 56 * 1024 * 1024

ROW_TILE = 256
MOE_TILE = 512
F_TILE = 512

f32 = jnp.float32
bf16 = jnp.bfloat16


def _params(*sem):
    return pltpu.CompilerParams(dimension_semantics=sem, vmem_limit_bytes=VMEM_LIMIT)


def _mod_spec(layer, k, ngrid):
    if ngrid == 1:
        return pl.BlockSpec((1, SUBLANES, D), lambda i: (layer, 0, k))
    return pl.BlockSpec((1, SUBLANES, D), lambda i, j: (layer, 0, k))


def _select_rows(m_ref, row0, tm, mixed):
    lat = m_ref[0, 0:1, :]
    if not mixed:
        return lat
    ctx = m_ref[0, 1:2, :]
    rows = row0 + lax.broadcasted_iota(jnp.int32, (tm, D), 0)
    return jnp.where(rows >= SEQ, ctx, lat)


def _norm_mod(x, g, shift, scale):
    y = x * lax.rsqrt(jnp.mean(x * x, axis=-1, keepdims=True) + EPS)
    return (y * g) * (1 + scale) + shift


def _mod_kernel(cb_ref, w_ref, b_ref, o_ref):
    s_lat = jax.nn.silu(cb_ref[0])
    s_ctx = jax.nn.silu(cb_ref[1])
    tn = w_ref.shape[2]
    o_ref[0] = jnp.zeros((SUBLANES, tn), f32)
    for j in range(tn // LANES):
        cols = slice(j * LANES, (j + 1) * LANES)
        wj = w_ref[0, :, cols]
        bj = b_ref[0, :, cols]
        o_ref[0, 0:1, cols] = jnp.sum(wj * s_lat, axis=0, keepdims=True) + bj
        o_ref[0, 1:2, cols] = jnp.sum(wj * s_ctx, axis=0, keepdims=True) + bj


def _mod_table(cb, w_mod, b_mod):
    tn = 1024
    return pl.pallas_call(
        _mod_kernel,
        out_shape=jax.ShapeDtypeStruct((DEPTH, SUBLANES, NMOD * D), f32),
        grid=(DEPTH, NMOD * D // tn),
        in_specs=[
            pl.BlockSpec((2, D, LANES), lambda l, j: (0, 0, 0)),
            pl.BlockSpec((1, D, tn), lambda l, j: (l, 0, j)),
            pl.BlockSpec((1, 1, tn), lambda l, j: (l, 0, j)),
        ],
        out_specs=pl.BlockSpec((1, SUBLANES, tn), lambda l, j: (l, 0, j)),
        compiler_params=_params("arbitrary", "arbitrary"),
        name="mod_table",
    )(cb, w_mod, b_mod.reshape(DEPTH, 1, NMOD * D))


def _inproj_kernel(x_ref, g_ref, sh_ref, sc_ref, w_ref, o_ref, h_ref, *, tm, mixed, act):
    i = pl.program_id(0)

    @pl.when(pl.program_id(1) == 0)
    def _():
        shift = _select_rows(sh_ref, i * tm, tm, mixed)
        scale = _select_rows(sc_ref, i * tm, tm, mixed)
        h_ref[...] = _norm_mod(x_ref[...], g_ref[...], shift, scale).astype(bf16)

    z = jnp.dot(h_ref[...], w_ref[...], preferred_element_type=f32)
    if act:
        z = jax.nn.gelu(z)
    o_ref[...] = z


def _inproj(xa, g, mod, layer, w, act):
    m = xa.shape[0]
    n = w.shape[1]
    tm = 768 if m % 768 == 0 else 1024
    tn = 1024
    kern = functools.partial(_inproj_kernel, tm=tm, mixed=m > SEQ, act=act)
    return pl.pallas_call(
        kern,
        out_shape=jax.ShapeDtypeStruct((m, n), f32),
        grid=(m // tm, n // tn),
        in_specs=[
            pl.BlockSpec((tm, D), lambda i, j: (i, 0)),
            pl.BlockSpec((1, D), lambda i, j: (0, 0)),
            _mod_spec(layer, 0, 2),
            _mod_spec(layer, 1, 2),
            pl.BlockSpec((D, tn), lambda i, j: (0, j)),
        ],
        out_specs=pl.BlockSpec((tm, tn), lambda i, j: (i, j)),
        scratch_shapes=[pltpu.VMEM((tm, D), bf16)],
        compiler_params=_params("arbitrary", "arbitrary"),
        name="mixer_inproj",
    )(xa, g, mod, mod, w)


def _lru_kernel(cf_ref, pf_ref, nf_ref, cr_ref, pr_ref, nr_ref, cw_ref, cb_ref,
                wa_ref, wx_ref, ba_ref, bx_ref, lam_ref, yf_ref, yr_ref, carry_ref,
                *, nlb, nb, tt):
    s = pl.program_id(1)

    @pl.when(s == 0)
    def _():
        carry_ref[...] = jnp.zeros_like(carry_ref)

    fblk = lax.rem(s + nb - 1, nb)
    rblk = nb - 1 - s
    row = lax.broadcasted_iota(jnp.int32, (tt, HD), 0)
    sub = row & (SUBLANES - 1)

    def conv(c_ref, p_ref, n_ref, blk):
        first = jnp.logical_or(blk == 0, blk == nlb)
        last = jnp.logical_or(blk == nlb - 1, blk == nb - 1)
        cur = c_ref[...]
        prev = jnp.where(first, 0.0, p_ref[...])
        nxt = jnp.where(last, 0.0, n_ref[...])
        x_m1 = jnp.where(row == 0, prev[7:8], pltpu.roll(cur, 1, 0))
        x_m2 = jnp.where(row == 0, prev[6:7],
                         jnp.where(row == 1, prev[7:8], pltpu.roll(cur, 2, 0)))
        x_p1 = jnp.where(row == tt - 1, nxt[0:1], pltpu.roll(cur, tt - 1, 0))
        y = cb_ref[...] + cw_ref[0:1] * x_m2
        y = y + cw_ref[1:2] * x_m1
        y = y + cw_ref[2:3] * cur
        y = y + cw_ref[3:4] * x_p1
        return y

    def coeffs(xc, d):
        xb = xc.astype(bf16)
        gate_r = jax.nn.sigmoid(jnp.dot(xb, wa_ref[d, 0], preferred_element_type=f32) + ba_ref[d:d + 1])
        gate_i = jax.nn.sigmoid(jnp.dot(xb, wx_ref[d, 0], preferred_element_type=f32) + bx_ref[d:d + 1])
        nlam = -lam_ref[d:d + 1]
        softplus = jnp.maximum(nlam, 0.0) + jnp.log1p(jnp.exp(-jnp.abs(nlam)))
        log_a = (-LRU_C * gate_r) * softplus
        a = jnp.exp(log_a)
        b = jnp.sqrt(1.0 - a * a) * gate_i * xc
        return a, b

    def scan(a, b, d, y_ref):
        rev = d == 1
        for sh in (1, 2, 4):
            if rev:
                a_s = pltpu.roll(a, tt - sh, 0)
                b_s = pltpu.roll(b, tt - sh, 0)
                m = sub < SUBLANES - sh
            else:
                a_s = pltpu.roll(a, sh, 0)
                b_s = pltpu.roll(b, sh, 0)
                m = sub >= sh
            b = jnp.where(m, a * b_s + b, b)
            a = jnp.where(m, a * a_s, a)
        c = carry_ref[d]
        ng = tt // SUBLANES
        for g in (range(ng - 1, -1, -1) if rev else range(ng)):
            rows = slice(g * SUBLANES, (g + 1) * SUBLANES)
            hg = a[rows] * c + b[rows]
            y_ref[rows, :] = hg
            edge = hg[0:1] if rev else hg[SUBLANES - 1:SUBLANES]
            c = jnp.broadcast_to(edge, (SUBLANES, HD))
        carry_ref[d] = c

    a_f, b_f = coeffs(conv(cf_ref, pf_ref, nf_ref, fblk), 0)
    scan(a_f, b_f, 0, yf_ref)
    a_r, b_r = coeffs(conv(cr_ref, pr_ref, nr_ref, rblk), 1)
    scan(a_r, b_r, 1, yr_ref)


def _lru_scan(z, conv_w, conv_b, w_a, w_x, b_a, b_x, lam):
    m = z.shape[0]
    tt = ROW_TILE
    assert CTX == tt and m == SEQ + CTX
    nb = m // tt
    nlb = SEQ // tt
    r8 = tt // SUBLANES
    nrow8 = m // SUBLANES

    def fblk(s):
        return lax.rem(s + nb - 1, nb)

    def rblk(s):
        return nb - 1 - s

    def cur(blk):
        return pl.BlockSpec((tt, HD), lambda h, s: (blk(s), h))

    def prev(blk):
        return pl.BlockSpec((SUBLANES, HD), lambda h, s: (jnp.maximum(blk(s) * r8 - 1, 0), h))

    def nxt(blk):
        return pl.BlockSpec((SUBLANES, HD), lambda h, s: (jnp.minimum((blk(s) + 1) * r8, nrow8 - 1), h))

    kern = functools.partial(_lru_kernel, nlb=nlb, nb=nb, tt=tt)
    vec2 = pl.BlockSpec((2, HD), lambda h, s: (0, h))
    gate_w = pl.BlockSpec((2, 1, HD, HD), lambda h, s: (0, h, 0, 0))
    return pl.pallas_call(
        kern,
        out_shape=(jax.ShapeDtypeStruct((m, W), f32), jax.ShapeDtypeStruct((m, W), f32)),
        grid=(HEADS, nb),
        in_specs=[
            cur(fblk), prev(fblk), nxt(fblk), cur(rblk), prev(rblk), nxt(rblk),
            pl.BlockSpec((CONV, HD), lambda h, s: (0, h)),
            pl.BlockSpec((1, HD), lambda h, s: (0, h)),
            gate_w, gate_w, vec2, vec2, vec2,
        ],
        out_specs=(pl.BlockSpec((tt, HD), lambda h, s: (fblk(s), h)),
                   pl.BlockSpec((tt, HD), lambda h, s: (rblk(s), h))),
        scratch_shapes=[pltpu.VMEM((2, SUBLANES, HD), f32)],
        compiler_params=_params("arbitrary", "arbitrary"),
        name="rglru_scan",
    )(z, z, z, z, z, z, conv_w, conv_b, w_a, w_x, b_a, b_x, lam)


def _lru_out_kernel(yf_ref, yr_ref, g_ref, x_ref, gate_ref, w_ref, o_ref, *, tm, mixed):
    a = ((yf_ref[...] + yr_ref[...]) * jax.nn.gelu(g_ref[...])).astype(bf16)
    mix = jnp.dot(a, w_ref[...], preferred_element_type=f32)
    gate = _select_rows(gate_ref, pl.program_id(0) * tm, tm, mixed)
    o_ref[...] = x_ref[...] + gate * mix


def _lru_out(yf, yr, z, xa, mod, layer, w_out, m_out):
    tm = ROW_TILE
    kern = functools.partial(_lru_out_kernel, tm=tm, mixed=m_out > SEQ)
    rows = pl.BlockSpec((tm, D), lambda i: (i, 0))
    return pl.pallas_call(
        kern,
        out_shape=jax.ShapeDtypeStruct((m_out, D), f32),
        grid=(m_out // tm,),
        in_specs=[
            rows, rows,
            pl.BlockSpec((tm, W), lambda i: (i, 1)),
            rows,
            _mod_spec(layer, 2, 1),
            pl.BlockSpec((W, D), lambda i: (0, 0)),
        ],
        out_specs=rows,
        compiler_params=_params("arbitrary"),
        name="rglru_out",
    )(yf, yr, z, xa, mod, w_out)


def _sgu_out_kernel(z_ref, lng_ref, lnb_ref, ws_ref, bst_ref, x_ref, gate_ref, w_ref,
                    o_ref, a_ref, *, tm, mixed):
    for c in range(tm // CHUNK):
        rows = slice(c * CHUNK, (c + 1) * CHUNK)
        v = z_ref[rows, W:]
        mu = jnp.mean(v, axis=-1, keepdims=True)
        vc = v - mu
        vn = vc * lax.rsqrt(jnp.mean(vc * vc, axis=-1, keepdims=True) + EPS)
        vn = (vn * lng_ref[...] + lnb_ref[...]).astype(bf16)
        for g in range(GROUPS):
            cols = slice(g * GD, (g + 1) * GD)
            vm = jnp.dot(ws_ref[g], vn[:, cols], preferred_element_type=f32) + bst_ref[:, g:g + 1]
            a_ref[rows, cols] = (z_ref[rows, cols] * vm).astype(bf16)
    mix = jnp.dot(a_ref[...], w_ref[...], preferred_element_type=f32)
    gate = _select_rows(gate_ref, pl.program_id(0) * tm, tm, mixed)
    o_ref[...] = x_ref[...] + gate * mix


def _sgu_out(z, ln_g, ln_b, w_s, b_s_t, xa, mod, layer, w_out):
    m = xa.shape[0]
    tm = ROW_TILE
    kern = functools.partial(_sgu_out_kernel, tm=tm, mixed=m > SEQ)
    rows = pl.BlockSpec((tm, D), lambda i: (i, 0))
    return pl.pallas_call(
        kern,
        out_shape=jax.ShapeDtypeStruct((m, D), f32),
        grid=(m // tm,),
        in_specs=[
            pl.BlockSpec((tm, 2 * W), lambda i: (i, 0)),
            pl.BlockSpec((1, W), lambda i: (0, 0)),
            pl.BlockSpec((1, W), lambda i: (0, 0)),
            pl.BlockSpec((GROUPS, CHUNK, CHUNK), lambda i: (0, 0, 0)),
            pl.BlockSpec((CHUNK, GROUPS), lambda i: (0, 0)),
            rows,
            _mod_spec(layer, 2, 1),
            pl.BlockSpec((W, D), lambda i: (0, 0)),
        ],
        out_specs=rows,
        scratch_shapes=[pltpu.VMEM((tm, W), bf16)],
        compiler_params=_params("arbitrary"),
        name="sgu_out",
    )(z, ln_g, ln_b, w_s, b_s_t, xa, mod, w_out)


def _ffn_kernel(x_ref, g_ref, sh_ref, sc_ref, gate_ref, w1_ref, w3_ref, w2_ref, o_ref, h_ref,
                *, tm, mixed, nf):
    i = pl.program_id(0)
    f = pl.program_id(1)

    @pl.when(f == 0)
    def _():
        shift = _select_rows(sh_ref, i * tm, tm, mixed)
        scale = _select_rows(sc_ref, i * tm, tm, mixed)
        h_ref[...] = _norm_mod(x_ref[...], g_ref[...], shift, scale).astype(bf16)
        o_ref[...] = jnp.zeros_like(o_ref)

    h = h_ref[...]
    t = jax.nn.silu(jnp.dot(h, w1_ref[...], preferred_element_type=f32))
    t = (t * jnp.dot(h, w3_ref[...], preferred_element_type=f32)).astype(bf16)
    o_ref[...] += jnp.dot(t, w2_ref[...], preferred_element_type=f32)

    @pl.when(f == nf - 1)
    def _():
        gate = _select_rows(gate_ref, i * tm, tm, mixed)
        o_ref[...] = x_ref[...] + gate * o_ref[...]


def _ffn(xa, g, mod, layer, w1, w3, w2):
    m = xa.shape[0]
    tm = 528 if m % 528 == 0 else 512
    tf = F_TILE
    nf = F // tf
    kern = functools.partial(_ffn_kernel, tm=tm, mixed=m > SEQ, nf=nf)
    return pl.pallas_call(
        kern,
        out_shape=jax.ShapeDtypeStruct((m, D), f32),
        grid=(m // tm, nf),
        in_specs=[
            pl.BlockSpec((tm, D), lambda i, f: (i, 0)),
            pl.BlockSpec((1, D), lambda i, f: (0, 0)),
            _mod_spec(layer, 3, 2),
            _mod_spec(layer, 4, 2),
            _mod_spec(layer, 5, 2),
            pl.BlockSpec((D, tf), lambda i, f: (0, f)),
            pl.BlockSpec((D, tf), lambda i, f: (0, f)),
            pl.BlockSpec((tf, D), lambda i, f: (f, 0)),
        ],
        out_specs=pl.BlockSpec((tm, D), lambda i, f: (i, 0)),
        scratch_shapes=[pltpu.VMEM((tm, D), bf16)],
        compiler_params=_params("arbitrary", "arbitrary"),
        name="ffn_swiglu",
    )(xa, g, mod, mod, mod, w1, w3, w2)


def _router_kernel(x_ref, g_ref, sh_ref, sc_ref, wr_ref, h_ref, route_ref, cnt_ref, run_ref,
                   *, tm, mixed):
    i = pl.program_id(0)

    @pl.when(i == 0)
    def _():
        run_ref[...] = jnp.zeros_like(run_ref)

    shift = _select_rows(sh_ref, i * tm, tm, mixed)
    scale = _select_rows(sc_ref, i * tm, tm, mixed)
    h = _norm_mod(x_ref[...], g_ref[...], shift, scale)
    h_ref[...] = h

    logits = jnp.dot(h, wr_ref[...], preferred_element_type=f32, precision=lax.Precision.HIGHEST)
    lane = lax.broadcasted_iota(jnp.int32, (tm, LANES), 1)
    lane_f = lane.astype(f32)
    neg = jnp.float32(-jnp.inf)
    logits = jnp.where(lane < E, logits, neg)
    m1 = jnp.max(logits, axis=-1, keepdims=True)
    i1 = jnp.min(jnp.where(logits == m1, lane_f, float(LANES)), axis=-1, keepdims=True)
    oh1 = lane_f == i1
    rest = jnp.where(oh1, neg, logits)
    m2 = jnp.max(rest, axis=-1, keepdims=True)
    i2 = jnp.min(jnp.where(rest == m2, lane_f, float(LANES)), axis=-1, keepdims=True)
    oh2 = lane_f == i2
    ex = jnp.exp(m2 - m1)
    den = 1.0 + ex
    g1 = 1.0 / den
    g2 = ex / den

    oh = jnp.where(jnp.logical_or(oh1, oh2), 1.0, 0.0)
    r_i = lax.broadcasted_iota(jnp.int32, (tm, tm), 0)
    c_i = lax.broadcasted_iota(jnp.int32, (tm, tm), 1)
    tri = jnp.where(r_i > c_i, 1.0, 0.0).astype(bf16)
    before = jnp.dot(tri, oh.astype(bf16), preferred_element_type=f32) + run_ref[0:1]
    r1 = jnp.sum(jnp.where(oh1, before, 0.0), axis=-1, keepdims=True)
    r2 = jnp.sum(jnp.where(oh2, before, 0.0), axis=-1, keepdims=True)
    run_ref[...] = run_ref[...] + jnp.sum(oh, axis=0, keepdims=True)
    cnt_ref[...] = run_ref[...]

    route = jnp.where(lane == 0, i1, 0.0)
    route = jnp.where(lane == 1, i2, route)
    route = jnp.where(lane == 2, g1, route)
    route = jnp.where(lane == 3, g2, route)
    route = jnp.where(lane == 4, r1, route)
    route = jnp.where(lane == 5, r2, route)
    route_ref[...] = route


def _router(xa, g, mod, layer, w_router_pad):
    m = xa.shape[0]
    tm = ROW_TILE
    kern = functools.partial(_router_kernel, tm=tm, mixed=m > SEQ)
    rows = pl.BlockSpec((tm, D), lambda i: (i, 0))
    return pl.pallas_call(
        kern,
        out_shape=(jax.ShapeDtypeStruct((m, D), f32),
                   jax.ShapeDtypeStruct((m, LANES), f32),
                   jax.ShapeDtypeStruct((SUBLANES, LANES), f32)),
        grid=(m // tm,),
        in_specs=[
            rows,
            pl.BlockSpec((1, D), lambda i: (0, 0)),
            _mod_spec(layer, 3, 1),
            _mod_spec(layer, 4, 1),
            pl.BlockSpec((D, LANES), lambda i: (0, 0)),
        ],
        out_specs=(rows,
                   pl.BlockSpec((tm, LANES), lambda i: (i, 0)),
                   pl.BlockSpec((SUBLANES, LANES), lambda i: (0, 0))),
        scratch_shapes=[pltpu.VMEM((SUBLANES, LANES), f32)],
        compiler_params=_params("arbitrary"),
        name="moe_router",
    )(xa, g, mod, mod, w_router_pad)


def _row_copy(src_hbm, src_row, dst_ref, dst_row, sem):
    return pltpu.make_async_copy(src_hbm.at[pl.ds(src_row, 1), :],
                                 dst_ref.at[pl.ds(dst_row, 1), :], sem)


def _experts_kernel(be_ref, nu_ref, tok_ref, h_hbm, w1_ref, w3_ref, w2_ref, o_ref,
                    xg_ref, xb_ref, sem, *, tme):
    del be_ref
    i = pl.program_id(0)
    f = pl.program_id(1)
    used = i < nu_ref[0]

    @pl.when(jnp.logical_and(used, f == 0))
    def _():
        def issue(r, carry):
            _row_copy(h_hbm, tok_ref[i * tme + r], xg_ref, r, sem).start()
            return carry

        def wait(r, carry):
            _row_copy(h_hbm, 0, xg_ref, r, sem).wait()
            return carry

        lax.fori_loop(0, tme, issue, 0)
        lax.fori_loop(0, tme, wait, 0)
        xb_ref[...] = xg_ref[...].astype(bf16)

    @pl.when(f == 0)
    def _():
        o_ref[...] = jnp.zeros_like(o_ref)

    @pl.when(used)
    def _():
        h = xb_ref[...]
        t = jax.nn.silu(jnp.dot(h, w1_ref[0], preferred_element_type=f32))
        t = (t * jnp.dot(h, w3_ref[0], preferred_element_type=f32)).astype(bf16)
        o_ref[...] += jnp.dot(t, w2_ref[0], preferred_element_type=f32)


def _experts(h, block_expert, n_used, tok, w1, w3, w2, nblk):
    tme = MOE_TILE
    tf = F_TILE
    nf = F // tf

    def fidx(i, f, nu):
        return jnp.where(i < nu[0], f, nf - 1)

    kern = functools.partial(_experts_kernel, tme=tme)
    grid_spec = pltpu.PrefetchScalarGridSpec(
        num_scalar_prefetch=3,
        grid=(nblk, nf),
        in_specs=[
            pl.BlockSpec(memory_space=pl.ANY),
            pl.BlockSpec((1, D, tf), lambda i, f, be, nu, tok: (be[i], 0, fidx(i, f, nu))),
            pl.BlockSpec((1, D, tf), lambda i, f, be, nu, tok: (be[i], 0, fidx(i, f, nu))),
            pl.BlockSpec((1, tf, D), lambda i, f, be, nu, tok: (be[i], fidx(i, f, nu), 0)),
        ],
        out_specs=pl.BlockSpec((tme, D), lambda i, f, be, nu, tok: (i, 0)),
        scratch_shapes=[pltpu.VMEM((tme, D), f32), pltpu.VMEM((tme, D), bf16),
                        pltpu.SemaphoreType.DMA(())],
    )
    return pl.pallas_call(
        kern,
        out_shape=jax.ShapeDtypeStruct((nblk * tme, D), f32),
        grid_spec=grid_spec,
        compiler_params=_params("arbitrary", "arbitrary"),
        name="moe_experts",
    )(block_expert, n_used, tok, h, w1, w3, w2)


def _combine_kernel(p1_ref, p2_ref, y_hbm, x_ref, route_ref, gate_ref, fg_ref, o_ref,
                    y1_ref, y2_ref, sem, *, tm, mixed, final):
    i = pl.program_id(0)

    def issue(r, carry):
        _row_copy(y_hbm, p1_ref[i * tm + r], y1_ref, r, sem).start()
        _row_copy(y_hbm, p2_ref[i * tm + r], y2_ref, r, sem).start()
        return carry

    def wait(r, carry):
        _row_copy(y_hbm, 0, y1_ref, r, sem).wait()
        _row_copy(y_hbm, 0, y2_ref, r, sem).wait()
        return carry

    lax.fori_loop(0, tm, issue, 0)
    lax.fori_loop(0, tm, wait, 0)
    y = route_ref[:, 2:3] * y1_ref[...] + route_ref[:, 3:4] * y2_ref[...]
    gate = _select_rows(gate_ref, i * tm, tm, mixed)
    out = x_ref[...] + gate * y
    if final:
        out = out * lax.rsqrt(jnp.mean(out * out, axis=-1, keepdims=True) + EPS) * fg_ref[...]
    o_ref[...] = out


def _combine(ybuf, pos1, pos2, xa, route, mod, layer, final_g, final):
    m = xa.shape[0]
    tm = ROW_TILE
    kern = functools.partial(_combine_kernel, tm=tm, mixed=m > SEQ, final=final)
    grid_spec = pltpu.PrefetchScalarGridSpec(
        num_scalar_prefetch=2,
        grid=(m // tm,),
        in_specs=[
            pl.BlockSpec(memory_space=pl.ANY),
            pl.BlockSpec((tm, D), lambda i, p1, p2: (i, 0)),
            pl.BlockSpec((tm, LANES), lambda i, p1, p2: (i, 0)),
            pl.BlockSpec((1, SUBLANES, D), lambda i, p1, p2: (layer, 0, 5)),
            pl.BlockSpec((1, D), lambda i, p1, p2: (0, 0)),
        ],
        out_specs=pl.BlockSpec((tm, D), lambda i, p1, p2: (i, 0)),
        scratch_shapes=[pltpu.VMEM((tm, D), f32), pltpu.VMEM((tm, D), f32),
                        pltpu.SemaphoreType.DMA(())],
    )
    return pl.pallas_call(
        kern,
        out_shape=jax.ShapeDtypeStruct((m, D), f32),
        grid_spec=grid_spec,
        compiler_params=_params("arbitrary"),
        name="moe_combine",
    )(pos1, pos2, ybuf, xa, route, mod, final_g)


def _moe(xa, g, mod, layer, w_router, w1, w3, w2, final_g, final):
    m = xa.shape[0]
    tme = MOE_TILE
    wr = jnp.zeros((D, LANES), f32).at[:, :E].set(w_router)
    h, route, cnt = _router(xa, g, mod, layer, wr)

    e1 = route[:, 0].astype(jnp.int32)
    e2 = route[:, 1].astype(jnp.int32)
    r1 = route[:, 4].astype(jnp.int32)
    r2 = route[:, 5].astype(jnp.int32)
    counts = cnt[0, :E].astype(jnp.int32)
    padded = ((counts + tme - 1) // tme) * tme
    pends = jnp.cumsum(padded)
    pstarts = pends - padded
    pos1 = pstarts[e1] + r1
    pos2 = pstarts[e2] + r2
    nblk = -(-(2 * m + E * (tme - 1)) // tme)
    n_used = pends[-1] // tme
    blk = jnp.minimum(jnp.arange(nblk, dtype=jnp.int32), n_used - 1)
    block_expert = jnp.clip(jnp.searchsorted(pends, blk * tme, side="right"), 0, E - 1).astype(jnp.int32)
    token = jnp.arange(m, dtype=jnp.int32)
    tok = jnp.zeros((nblk * tme,), jnp.int32).at[pos1].set(token).at[pos2].set(token)

    ybuf = _experts(h, block_expert, n_used.reshape(1).astype(jnp.int32), tok, w1, w3, w2, nblk)
    return _combine(ybuf, pos1, pos2, xa, route, mod, layer, final_g, final)


def kernel(x, c, ctx, c_ctx, norm_g, final_g, w_mod, b_mod, lru_w_in, lru_conv_w, lru_conv_b, lru_w_a, lru_b_a, lru_w_x, lru_b_x, lru_lambda, lru_w_out, sgu_w_in, sgu_ln_g, sgu_ln_b, sgu_w_s, sgu_b_s, sgu_w_out, ffn_w1, ffn_w3, ffn_w2, moe_router, moe_w1, moe_w3, moe_w2):
    assert x.shape == (1, SEQ, D) and ctx.shape == (1, CTX, D)
    xa = jnp.concatenate([x[0], ctx[0]], axis=0)
    cb = jnp.stack([jnp.broadcast_to(c[0][:, None], (D, LANES)),
                    jnp.broadcast_to(c_ctx[:, None], (D, LANES))])
    mod = _mod_table(cb, w_mod, b_mod)
    fg = final_g.reshape(1, D)

    for layer in range(DEPTH):
        j = layer // 2
        ctx_next = layer < 2
        g_mix = norm_g[layer, 0].reshape(1, D)
        g_ch = norm_g[layer, 1].reshape(1, D)
        if layer % 2 == 0:
            z = _inproj(xa, g_mix, mod, layer, lru_w_in[j].astype(bf16), act=False)
            yf, yr = _lru_scan(z, lru_conv_w[j], lru_conv_b[j].reshape(1, W),
                               lru_w_a[j].astype(bf16), lru_w_x[j].astype(bf16),
                               lru_b_a[j].reshape(2, W), lru_b_x[j].reshape(2, W), lru_lambda[j])
            m_out = SEQ + CTX if ctx_next else SEQ
            xa = _lru_out(yf, yr, z, xa, mod, layer, lru_w_out[j].astype(bf16), m_out)
            xa = _ffn(xa, g_ch, mod, layer, ffn_w1[j].astype(bf16), ffn_w3[j].astype(bf16),
                      ffn_w2[j].astype(bf16))
        else:
            z = _inproj(xa, g_mix, mod, layer, sgu_w_in[j].astype(bf16), act=True)
            xa = _sgu_out(z, sgu_ln_g[j].reshape(1, W), sgu_ln_b[j].reshape(1, W),
                          sgu_w_s[j].astype(bf16), sgu_b_s[j].T, xa, mod, layer,
                          sgu_w_out[j].astype(bf16))
            xa = _moe(xa, g_ch, mod, layer, moe_router[j], moe_w1[j].astype(bf16),
                      moe_w3[j].astype(bf16), moe_w2[j].astype(bf16), fg, final=layer == DEPTH - 1)
    return xa[None]
```

```python
import functools

import jax
import jax.numpy as jnp
from jax import lax
from jax.experimental import pallas as pl
from jax.experimental.pallas import tpu as pltpu

D = 2048
SEQ = 8192
CTX = 256
DEPTH = 4
W = D
HEADS = 8
HD = W // HEADS
CONV = 4
LRU_C = 8.0
CHUNK = 128
GROUPS = 8
GD = W // GROUPS
F = 5632
E = 8
NMOD = 6
EPS = 1e-6

LANES = 128
SUBLANES = 8
VMEM_LIMIT = 56 * 1024 * 1024

ROW_TILE = 256
NORM_ROWS = 32
W_CHUNK = 256
F_TILE = 256
MOE_TILE = 1024
MOE_SUB = 256
NF = F // F_TILE
GATHER_PER_STEP = -(-MOE_TILE // (NF * SUBLANES)) * SUBLANES
GATHER_ROWS = GATHER_PER_STEP * NF

f32 = jnp.float32
bf16 = jnp.bfloat16


def _params(*sem):
    return pltpu.CompilerParams(dimension_semantics=sem, vmem_limit_bytes=VMEM_LIMIT)


def _mod_spec(layer, k, ngrid):
    if ngrid == 1:
        return pl.BlockSpec((1, SUBLANES, D), lambda i: (layer, 0, k))
    return pl.BlockSpec((1, SUBLANES, D), lambda i, j: (layer, 0, k))


def _block_mod(m_ref, row0, tm, mixed):
    lat = m_ref[0, 0:1, :]
    if not mixed:
        return lat
    assert SEQ % tm == 0
    return jnp.where(row0 >= SEQ, m_ref[0, 1:2, :], lat)


def _rows_mod(m_ref, row0, rows, mixed):
    lat = m_ref[0, 0:1, :]
    if not mixed:
        return lat
    ridx = row0 + lax.broadcasted_iota(jnp.int32, (rows, D), 0)
    return jnp.where(ridx >= SEQ, m_ref[0, 1:2, :], lat)


def _norm_mod(x, g, shift, scale):
    y = x * lax.rsqrt(jnp.mean(x * x, axis=-1, keepdims=True) + EPS)
    return (y * g) * (1 + scale) + shift


def _norm_mod_rows(x_ref, g_ref, sh_ref, sc_ref, out_ref, row0, tm, mixed):
    g = g_ref[...]
    uniform = (not mixed) or SEQ % tm == 0
    if uniform:
        shift_b = _block_mod(sh_ref, row0, tm, mixed)
        scale_b = _block_mod(sc_ref, row0, tm, mixed)

    def body(c, carry):
        r = pl.multiple_of(c * NORM_ROWS, NORM_ROWS)
        if uniform:
            shift, scale = shift_b, scale_b
        else:
            shift = _rows_mod(sh_ref, row0 + r, NORM_ROWS, mixed)
            scale = _rows_mod(sc_ref, row0 + r, NORM_ROWS, mixed)
        x = x_ref[pl.ds(r, NORM_ROWS), :]
        out_ref[pl.ds(r, NORM_ROWS), :] = _norm_mod(x, g, shift, scale).astype(out_ref.dtype)
        return carry

    lax.fori_loop(0, tm // NORM_ROWS, body, 0, unroll=2)


def _load_weight_bf16(w_hbm, w_ref, stage_ref, sem):
    k_rows = w_ref.shape[0]
    nchunks = k_rows // W_CHUNK

    def cp(k):
        return pltpu.make_async_copy(w_hbm.at[pl.ds(k * W_CHUNK, W_CHUNK), :],
                                     stage_ref.at[k % 2], sem.at[k % 2])

    cp(0).start()
    for k in range(nchunks):
        if k + 1 < nchunks:
            cp(k + 1).start()
        cp(k).wait()
        w_ref[k * W_CHUNK:(k + 1) * W_CHUNK, :] = stage_ref[k % 2].astype(bf16)


def _weight_scratch(k_rows, n_cols):
    return [pltpu.VMEM((k_rows, n_cols), bf16), pltpu.VMEM((2, W_CHUNK, n_cols), f32),
            pltpu.SemaphoreType.DMA((2,))]


def _mod_kernel(cb_ref, w_ref, b_ref, o_ref):
    s_lat = jax.nn.silu(cb_ref[0])
    s_ctx = jax.nn.silu(cb_ref[1])
    tn = w_ref.shape[2]
    o_ref[0] = jnp.zeros((SUBLANES, tn), f32)
    for j in range(tn // LANES):
        cols = slice(j * LANES, (j + 1) * LANES)
        wj = w_ref[0, :, cols]
        bj = b_ref[0, :, cols]
        o_ref[0, 0:1, cols] = jnp.sum(wj * s_lat, axis=0, keepdims=True) + bj
        o_ref[0, 1:2, cols] = jnp.sum(wj * s_ctx, axis=0, keepdims=True) + bj


def _mod_table(cb, w_mod, b_mod):
    tn = 1024
    return pl.pallas_call(
        _mod_kernel,
        out_shape=jax.ShapeDtypeStruct((DEPTH, SUBLANES, NMOD * D), f32),
        grid=(DEPTH, NMOD * D // tn),
        in_specs=[
            pl.BlockSpec((2, D, LANES), lambda l, j: (0, 0, 0)),
            pl.BlockSpec((1, D, tn), lambda l, j: (l, 0, j)),
            pl.BlockSpec((1, 1, tn), lambda l, j: (l, 0, j)),
        ],
        out_specs=pl.BlockSpec((1, SUBLANES, tn), lambda l, j: (l, 0, j)),
        compiler_params=_params("arbitrary", "arbitrary"),
        name="mod_table",
    )(cb, w_mod, b_mod.reshape(DEPTH, 1, NMOD * D))


def _inproj_kernel(x_ref, g_ref, sh_ref, sc_ref, w_hbm, o_ref, h_ref, w_ref, stage_ref, sem,
                   *, j, tm, mixed, act):
    i = pl.program_id(0)

    @pl.when(i == 0)
    def _():
        _load_weight_bf16(w_hbm.at[j], w_ref, stage_ref, sem)

    _norm_mod_rows(x_ref, g_ref, sh_ref, sc_ref, h_ref, i * tm, tm, mixed)
    z = jnp.dot(h_ref[...], w_ref[...], preferred_element_type=f32)
    if act:
        z = jax.nn.gelu(z)
    o_ref[...] = z


def _inproj(xa, g, mod, layer, w_stack, j, act):
    m = xa.shape[0]
    n = w_stack.shape[2]
    tm = ROW_TILE
    kern = functools.partial(_inproj_kernel, j=j, tm=tm, mixed=m > SEQ, act=act)
    return pl.pallas_call(
        kern,
        out_shape=jax.ShapeDtypeStruct((m, n), f32),
        grid=(m // tm,),
        in_specs=[
            pl.BlockSpec((tm, D), lambda i: (i, 0)),
            pl.BlockSpec((1, D), lambda i: (0, 0)),
            _mod_spec(layer, 0, 1),
            _mod_spec(layer, 1, 1),
            pl.BlockSpec(memory_space=pl.ANY),
        ],
        out_specs=pl.BlockSpec((tm, n), lambda i: (i, 0)),
        scratch_shapes=[pltpu.VMEM((tm, D), bf16)] + _weight_scratch(D, n),
        compiler_params=_params("arbitrary"),
        name="mixer_inproj",
    )(xa, g, mod, mod, w_stack)


def _lru_kernel(cf_ref, pf_ref, nf_ref, cr_ref, pr_ref, nr_ref, cw_ref, cb_ref,
                wa_ref, wx_ref, ba_ref, bx_ref, lam_ref, yf_ref, yr_ref, carry_ref,
                *, nlb, nb, tt):
    s = pl.program_id(1)

    @pl.when(s == 0)
    def _():
        carry_ref[...] = jnp.zeros_like(carry_ref)

    fblk = lax.rem(s + nb - 1, nb)
    rblk = nb - 1 - s
    row = lax.broadcasted_iota(jnp.int32, (tt, HD), 0)
    sub = row & (SUBLANES - 1)

    def conv(c_ref, p_ref, n_ref, blk):
        first = jnp.logical_or(blk == 0, blk == nlb)
        last = jnp.logical_or(blk == nlb - 1, blk == nb - 1)
        cur = c_ref[...]
        prev = jnp.where(first, 0.0, p_ref[...])
        nxt = jnp.where(last, 0.0, n_ref[...])
        x_m1 = jnp.where(row == 0, prev[7:8], pltpu.roll(cur, 1, 0))
        x_m2 = jnp.where(row == 0, prev[6:7],
                         jnp.where(row == 1, prev[7:8], pltpu.roll(cur, 2, 0)))
        x_p1 = jnp.where(row == tt - 1, nxt[0:1], pltpu.roll(cur, tt - 1, 0))
        y = cb_ref[...] + cw_ref[0:1] * x_m2
        y = y + cw_ref[1:2] * x_m1
        y = y + cw_ref[2:3] * cur
        y = y + cw_ref[3:4] * x_p1
        return y

    def coeffs(xc, d):
        xb = xc.astype(bf16)
        wa = wa_ref[d, 0].astype(bf16)
        wx = wx_ref[d, 0].astype(bf16)
        gate_r = jax.nn.sigmoid(jnp.dot(xb, wa, preferred_element_type=f32) + ba_ref[d:d + 1])
        gate_i = jax.nn.sigmoid(jnp.dot(xb, wx, preferred_element_type=f32) + bx_ref[d:d + 1])
        nlam = -lam_ref[d:d + 1]
        softplus = jnp.maximum(nlam, 0.0) + jnp.log1p(jnp.exp(-jnp.abs(nlam)))
        log_a = (-LRU_C * gate_r) * softplus
        a = jnp.exp(log_a)
        b = jnp.sqrt(1.0 - a * a) * gate_i * xc
        return a, b

    def scan(a, b, d, y_ref):
        rev = d == 1
        for sh in (1, 2, 4):
            if rev:
                a_s = pltpu.roll(a, tt - sh, 0)
                b_s = pltpu.roll(b, tt - sh, 0)
                m = sub < SUBLANES - sh
            else:
                a_s = pltpu.roll(a, sh, 0)
                b_s = pltpu.roll(b, sh, 0)
                m = sub >= sh
            b = jnp.where(m, a * b_s + b, b)
            a = jnp.where(m, a * a_s, a)
        c = carry_ref[d]
        ng = tt // SUBLANES
        for g in (range(ng - 1, -1, -1) if rev else range(ng)):
            rows = slice(g * SUBLANES, (g + 1) * SUBLANES)
            hg = a[rows] * c + b[rows]
            y_ref[rows, :] = hg
            edge = hg[0:1] if rev else hg[SUBLANES - 1:SUBLANES]
            c = jnp.broadcast_to(edge, (SUBLANES, HD))
        carry_ref[d] = c

    a_f, b_f = coeffs(conv(cf_ref, pf_ref, nf_ref, fblk), 0)
    scan(a_f, b_f, 0, yf_ref)
    a_r, b_r = coeffs(conv(cr_ref, pr_ref, nr_ref, rblk), 1)
    scan(a_r, b_r, 1, yr_ref)


def _lru_scan(z, j, conv_w, conv_b, w_a, w_x, b_a, b_x, lam):
    m = z.shape[0]
    tt = ROW_TILE
    assert CTX == tt and m == SEQ + CTX
    nb = m // tt
    nlb = SEQ // tt
    r8 = tt // SUBLANES
    nrow8 = m // SUBLANES

    def fblk(s):
        return lax.rem(s + nb - 1, nb)

    def rblk(s):
        return nb - 1 - s

    def cur(blk):
        return pl.BlockSpec((tt, HD), lambda h, s: (blk(s), h))

    def prev(blk):
        return pl.BlockSpec((SUBLANES, HD), lambda h, s: (jnp.maximum(blk(s) * r8 - 1, 0), h))

    def nxt(blk):
        return pl.BlockSpec((SUBLANES, HD), lambda h, s: (jnp.minimum((blk(s) + 1) * r8, nrow8 - 1), h))

    kern = functools.partial(_lru_kernel, nlb=nlb, nb=nb, tt=tt)
    vec2 = pl.BlockSpec((2, HD), lambda h, s: (0, h))
    gate_w = pl.BlockSpec((None, 2, 1, HD, HD), lambda h, s: (j, 0, h, 0, 0))
    return pl.pallas_call(
        kern,
        out_shape=(jax.ShapeDtypeStruct((m, W), f32), jax.ShapeDtypeStruct((m, W), f32)),
        grid=(HEADS, nb),
        in_specs=[
            cur(fblk), prev(fblk), nxt(fblk), cur(rblk), prev(rblk), nxt(rblk),
            pl.BlockSpec((CONV, HD), lambda h, s: (0, h)),
            pl.BlockSpec((1, HD), lambda h, s: (0, h)),
            gate_w, gate_w, vec2, vec2, vec2,
        ],
        out_specs=(pl.BlockSpec((tt, HD), lambda h, s: (fblk(s), h)),
                   pl.BlockSpec((tt, HD), lambda h, s: (rblk(s), h))),
        scratch_shapes=[pltpu.VMEM((2, SUBLANES, HD), f32)],
        compiler_params=_params("arbitrary", "arbitrary"),
        name="rglru_scan",
    )(z, z, z, z, z, z, conv_w, conv_b, w_a, w_x, b_a, b_x, lam)


def _lru_out_kernel(yf_ref, yr_ref, g_ref, x_ref, gate_ref, w_hbm, o_ref, w_ref, stage_ref, sem,
                    *, j, tm, mixed):
    i = pl.program_id(0)

    @pl.when(i == 0)
    def _():
        _load_weight_bf16(w_hbm.at[j], w_ref, stage_ref, sem)

    a = ((yf_ref[...] + yr_ref[...]) * jax.nn.gelu(g_ref[...])).astype(bf16)
    mix = jnp.dot(a, w_ref[...], preferred_element_type=f32)
    o_ref[...] = x_ref[...] + _block_mod(gate_ref, i * tm, tm, mixed) * mix


def _lru_out(yf, yr, z, xa, mod, layer, w_stack, j, m_out):
    tm = ROW_TILE
    kern = functools.partial(_lru_out_kernel, j=j, tm=tm, mixed=m_out > SEQ)
    rows = pl.BlockSpec((tm, D), lambda i: (i, 0))
    return pl.pallas_call(
        kern,
        out_shape=jax.ShapeDtypeStruct((m_out, D), f32),
        grid=(m_out // tm,),
        in_specs=[
            rows, rows,
            pl.BlockSpec((tm, W), lambda i: (i, 1)),
            rows,
            _mod_spec(layer, 2, 1),
            pl.BlockSpec(memory_space=pl.ANY),
        ],
        out_specs=rows,
        scratch_shapes=_weight_scratch(W, D),
        compiler_params=_params("arbitrary"),
        name="rglru_out",
    )(yf, yr, z, xa, mod, w_stack)


def _sgu_out_kernel(z_ref, lng_ref, lnb_ref, ws_ref, bst_ref, x_ref, gate_ref, w_hbm,
                    o_ref, a_ref, w_ref, stage_ref, sem, *, j, tm, mixed):
    i = pl.program_id(0)

    @pl.when(i == 0)
    def _():
        _load_weight_bf16(w_hbm.at[j], w_ref, stage_ref, sem)

    for c in range(tm // CHUNK):
        rows = slice(c * CHUNK, (c + 1) * CHUNK)
        v = z_ref[rows, W:]
        mu = jnp.mean(v, axis=-1, keepdims=True)
        vc = v - mu
        vn = vc * lax.rsqrt(jnp.mean(vc * vc, axis=-1, keepdims=True) + EPS)
        vn = (vn * lng_ref[...] + lnb_ref[...]).astype(bf16)
        for g in range(GROUPS):
            cols = slice(g * GD, (g + 1) * GD)
            vm = jnp.dot(ws_ref[g].astype(bf16), vn[:, cols], preferred_element_type=f32)
            vm = vm + bst_ref[:, g:g + 1]
            a_ref[rows, cols] = (z_ref[rows, cols] * vm).astype(bf16)
    mix = jnp.dot(a_ref[...], w_ref[...], preferred_element_type=f32)
    o_ref[...] = x_ref[...] + _block_mod(gate_ref, i * tm, tm, mixed) * mix


def _sgu_out(z, j, ln_g, ln_b, w_s, b_s_t, xa, mod, layer, w_stack):
    m = xa.shape[0]
    tm = ROW_TILE
    kern = functools.partial(_sgu_out_kernel, j=j, tm=tm, mixed=m > SEQ)
    rows = pl.BlockSpec((tm, D), lambda i: (i, 0))
    return pl.pallas_call(
        kern,
        out_shape=jax.ShapeDtypeStruct((m, D), f32),
        grid=(m // tm,),
        in_specs=[
            pl.BlockSpec((tm, 2 * W), lambda i: (i, 0)),
            pl.BlockSpec((1, W), lambda i: (0, 0)),
            pl.BlockSpec((1, W), lambda i: (0, 0)),
            pl.BlockSpec((None, GROUPS, CHUNK, CHUNK), lambda i: (j, 0, 0, 0)),
            pl.BlockSpec((CHUNK, GROUPS), lambda i: (0, 0)),
            rows,
            _mod_spec(layer, 2, 1),
            pl.BlockSpec(memory_space=pl.ANY),
        ],
        out_specs=rows,
        scratch_shapes=[pltpu.VMEM((tm, W), bf16)] + _weight_scratch(W, D),
        compiler_params=_params("arbitrary"),
        name="sgu_out",
    )(z, ln_g, ln_b, w_s, b_s_t, xa, mod, w_stack)


def _swiglu_acc(h, w1, w3, w2, acc_ref, rows):
    t = jax.nn.silu(jnp.dot(h, w1, preferred_element_type=f32))
    t = (t * jnp.dot(h, w3, preferred_element_type=f32)).astype(bf16)
    acc_ref[rows, :] += jnp.dot(t, w2, preferred_element_type=f32)


def _ffn_kernel(x_ref, g_ref, sh_ref, sc_ref, gate_ref, w1_ref, w3_ref, w2_ref, o_ref, h_ref,
                *, tm, mixed):
    i = pl.program_id(0)
    f = pl.program_id(1)

    @pl.when(f == 0)
    def _():
        _norm_mod_rows(x_ref, g_ref, sh_ref, sc_ref, h_ref, i * tm, tm, mixed)
        o_ref[...] = jnp.zeros_like(o_ref)

    _swiglu_acc(h_ref[...], w1_ref[...].astype(bf16), w3_ref[...].astype(bf16),
                w2_ref[...].astype(bf16), o_ref, slice(None))

    @pl.when(f == NF - 1)
    def _():
        def body(c, carry):
            r = pl.multiple_of(c * NORM_ROWS, NORM_ROWS)
            rows = pl.ds(r, NORM_ROWS)
            gate = _rows_mod(gate_ref, i * tm + r, NORM_ROWS, mixed)
            o_ref[rows, :] = x_ref[rows, :] + gate * o_ref[rows, :]
            return carry

        lax.fori_loop(0, tm // NORM_ROWS, body, 0, unroll=2)


def _ffn(xa, g, mod, layer, w1, w3, w2, j):
    m = xa.shape[0]
    tm = 1056 if m % 1056 == 0 else 1024
    kern = functools.partial(_ffn_kernel, tm=tm, mixed=m > SEQ)
    return pl.pallas_call(
        kern,
        out_shape=jax.ShapeDtypeStruct((m, D), f32),
        grid=(m // tm, NF),
        in_specs=[
            pl.BlockSpec((tm, D), lambda i, f: (i, 0), pipeline_mode=pl.Buffered(1)),
            pl.BlockSpec((1, D), lambda i, f: (0, 0)),
            _mod_spec(layer, 3, 2),
            _mod_spec(layer, 4, 2),
            _mod_spec(layer, 5, 2),
            pl.BlockSpec((None, D, F_TILE), lambda i, f: (j, 0, f)),
            pl.BlockSpec((None, D, F_TILE), lambda i, f: (j, 0, f)),
            pl.BlockSpec((None, F_TILE, D), lambda i, f: (j, f, 0)),
        ],
        out_specs=pl.BlockSpec((tm, D), lambda i, f: (i, 0)),
        scratch_shapes=[pltpu.VMEM((tm, D), bf16)],
        compiler_params=_params("arbitrary", "arbitrary"),
        name="ffn_swiglu",
    )(xa, g, mod, mod, mod, w1, w3, w2)


def _router_kernel(x_ref, g_ref, sh_ref, sc_ref, wr_ref, h_ref, route_ref, cnt_ref, run_ref,
                   *, tm, mixed):
    i = pl.program_id(0)

    @pl.when(i == 0)
    def _():
        run_ref[...] = jnp.zeros_like(run_ref)

    _norm_mod_rows(x_ref, g_ref, sh_ref, sc_ref, h_ref, i * tm, tm, mixed)

    logits = jnp.dot(h_ref[...], wr_ref[...], preferred_element_type=f32,
                     precision=lax.Precision.HIGHEST)
    lane = lax.broadcasted_iota(jnp.int32, (tm, LANES), 1)
    lane_f = lane.astype(f32)
    neg = jnp.float32(-jnp.inf)
    logits = jnp.where(lane < E, logits, neg)
    m1 = jnp.max(logits, axis=-1, keepdims=True)
    i1 = jnp.min(jnp.where(logits == m1, lane_f, float(LANES)), axis=-1, keepdims=True)
    oh1 = lane_f == i1
    rest = jnp.where(oh1, neg, logits)
    m2 = jnp.max(rest, axis=-1, keepdims=True)
    i2 = jnp.min(jnp.where(rest == m2, lane_f, float(LANES)), axis=-1, keepdims=True)
    oh2 = lane_f == i2
    ex = jnp.exp(m2 - m1)
    den = 1.0 + ex
    g1 = 1.0 / den
    g2 = ex / den

    oh = jnp.where(jnp.logical_or(oh1, oh2), 1.0, 0.0)
    r_i = lax.broadcasted_iota(jnp.int32, (tm, tm), 0)
    c_i = lax.broadcasted_iota(jnp.int32, (tm, tm), 1)
    tri = jnp.where(r_i > c_i, 1.0, 0.0).astype(bf16)
    before = jnp.dot(tri, oh.astype(bf16), preferred_element_type=f32) + run_ref[0:1]
    r1 = jnp.sum(jnp.where(oh1, before, 0.0), axis=-1, keepdims=True)
    r2 = jnp.sum(jnp.where(oh2, before, 0.0), axis=-1, keepdims=True)
    run_ref[...] = run_ref[...] + jnp.sum(oh, axis=0, keepdims=True)
    cnt_ref[...] = run_ref[...]

    route = jnp.where(lane == 0, i1, 0.0)
    route = jnp.where(lane == 1, i2, route)
    route = jnp.where(lane == 2, g1, route)
    route = jnp.where(lane == 3, g2, route)
    route = jnp.where(lane == 4, r1, route)
    route = jnp.where(lane == 5, r2, route)
    route_ref[...] = route


def _router(xa, g, mod, layer, w_router_pad):
    m = xa.shape[0]
    tm = ROW_TILE
    kern = functools.partial(_router_kernel, tm=tm, mixed=m > SEQ)
    rows = pl.BlockSpec((tm, D), lambda i: (i, 0))
    return pl.pallas_call(
        kern,
        out_shape=(jax.ShapeDtypeStruct((m, D), f32),
                   jax.ShapeDtypeStruct((m, LANES), f32),
                   jax.ShapeDtypeStruct((SUBLANES, LANES), f32)),
        grid=(m // tm,),
        in_specs=[
            rows,
            pl.BlockSpec((1, D), lambda i: (0, 0)),
            _mod_spec(layer, 3, 1),
            _mod_spec(layer, 4, 1),
            pl.BlockSpec((D, LANES), lambda i: (0, 0)),
        ],
        out_specs=(rows,
                   pl.BlockSpec((tm, LANES), lambda i: (i, 0)),
                   pl.BlockSpec((SUBLANES, LANES), lambda i: (0, 0))),
        scratch_shapes=[pltpu.VMEM((SUBLANES, LANES), f32)],
        compiler_params=_params("arbitrary"),
        name="moe_router",
    )(xa, g, mod, mod, w_router_pad)


def _row_copy(src_hbm, src_row, dst_ref, dst_row, sem):
    return pltpu.make_async_copy(src_hbm.at[pl.ds(src_row, 1), :],
                                 dst_ref.at[pl.ds(dst_row, 1), :], sem)


def _experts_kernel(be_ref, nv_ref, tok_ref, h_hbm, w1_ref, w3_ref, w2_ref, o_ref,
                    xg_ref, xb_ref, w1b_ref, w3b_ref, w2b_ref, sem, *, nblk):
    del be_ref
    i = pl.program_id(0)
    f = pl.program_id(1)
    nvalid = nv_ref[i]
    used = nvalid > 0
    prev_used = nv_ref[jnp.maximum(i - 1, 0)] > 0
    first = jnp.logical_and(i == 0, used)

    def wait_row(r, carry):
        _row_copy(h_hbm, 0, xg_ref, r, sem).wait()
        return carry

    @pl.when(jnp.logical_and(f == 0, first))
    def _():
        def issue(r, carry):
            _row_copy(h_hbm, tok_ref[r], xg_ref, r, sem).start()
            return carry

        lax.fori_loop(0, GATHER_ROWS, issue, 0, unroll=8)

    @pl.when(jnp.logical_and(f == 0, jnp.logical_or(first, jnp.logical_and(i > 0, prev_used))))
    def _():
        lax.fori_loop(0, GATHER_ROWS, wait_row, 0, unroll=8)

        def cast(c, carry):
            rows = pl.ds(pl.multiple_of(c * NORM_ROWS, NORM_ROWS), NORM_ROWS)
            xb_ref[rows, :] = xg_ref[rows, :].astype(bf16)
            return carry

        lax.fori_loop(0, MOE_TILE // NORM_ROWS, cast, 0, unroll=2)

    @pl.when(f == 0)
    def _():
        o_ref[...] = jnp.zeros_like(o_ref)

    def gather_next_and_cast_weights():
        nxt = jnp.minimum(i + 1, nblk - 1) * MOE_TILE + f * GATHER_PER_STEP
        for k in range(GATHER_PER_STEP):
            _row_copy(h_hbm, tok_ref[nxt + k], xg_ref, f * GATHER_PER_STEP + k, sem).start()
        w1b_ref[...] = w1_ref[...].astype(bf16)
        w3b_ref[...] = w3_ref[...].astype(bf16)
        w2b_ref[...] = w2_ref[...].astype(bf16)

    @pl.when(nvalid == MOE_TILE)
    def _():
        gather_next_and_cast_weights()
        _swiglu_acc(xb_ref[...], w1b_ref[...], w3b_ref[...], w2b_ref[...], o_ref, slice(None))

    @pl.when(jnp.logical_and(used, nvalid < MOE_TILE))
    def _():
        gather_next_and_cast_weights()
        for s in range(MOE_TILE // MOE_SUB):
            @pl.when(s * MOE_SUB < nvalid)
            def _():
                rows = slice(s * MOE_SUB, (s + 1) * MOE_SUB)
                _swiglu_acc(xb_ref[rows, :], w1b_ref[...], w3b_ref[...], w2b_ref[...], o_ref, rows)

    @pl.when(jnp.logical_and(jnp.logical_and(i == nblk - 1, f == NF - 1), used))
    def _():
        lax.fori_loop(0, GATHER_ROWS, wait_row, 0, unroll=8)


def _experts(h, block_expert, nvalid, tok, w1, w3, w2, j, nblk):
    def fidx(i, f, nv):
        return jnp.where(nv[i] > 0, f, NF - 1)

    kern = functools.partial(_experts_kernel, nblk=nblk)
    grid_spec = pltpu.PrefetchScalarGridSpec(
        num_scalar_prefetch=3,
        grid=(nblk, NF),
        in_specs=[
            pl.BlockSpec(memory_space=pl.ANY),
            pl.BlockSpec((None, None, D, F_TILE), lambda i, f, be, nv, tok: (j, be[i], 0, fidx(i, f, nv))),
            pl.BlockSpec((None, None, D, F_TILE), lambda i, f, be, nv, tok: (j, be[i], 0, fidx(i, f, nv))),
            pl.BlockSpec((None, None, F_TILE, D), lambda i, f, be, nv, tok: (j, be[i], fidx(i, f, nv), 0)),
        ],
        out_specs=pl.BlockSpec((MOE_TILE, D), lambda i, f, be, nv, tok: (i, 0)),
        scratch_shapes=[pltpu.VMEM((GATHER_ROWS, D), f32), pltpu.VMEM((MOE_TILE, D), bf16),
                        pltpu.VMEM((D, F_TILE), bf16), pltpu.VMEM((D, F_TILE), bf16),
                        pltpu.VMEM((F_TILE, D), bf16), pltpu.SemaphoreType.DMA(())],
    )
    return pl.pallas_call(
        kern,
        out_shape=jax.ShapeDtypeStruct((nblk * MOE_TILE, D), f32),
        grid_spec=grid_spec,
        compiler_params=_params("arbitrary", "arbitrary"),
        name="moe_experts",
    )(block_expert, nvalid, tok, h, w1, w3, w2)


def _combine_kernel(p1_ref, p2_ref, y_hbm, x_ref, route_ref, gate_ref, fg_ref, o_ref,
                    y1_ref, y2_ref, sem, *, tm, mixed, final):
    i = pl.program_id(0)

    def issue(r, carry):
        _row_copy(y_hbm, p1_ref[i * tm + r], y1_ref, r, sem).start()
        _row_copy(y_hbm, p2_ref[i * tm + r], y2_ref, r, sem).start()
        return carry

    def wait(r, carry):
        _row_copy(y_hbm, 0, y1_ref, r, sem).wait()
        _row_copy(y_hbm, 0, y2_ref, r, sem).wait()
        return carry

    lax.fori_loop(0, tm, issue, 0, unroll=8)
    lax.fori_loop(0, tm, wait, 0, unroll=8)
    y = route_ref[:, 2:3] * y1_ref[...] + route_ref[:, 3:4] * y2_ref[...]
    out = x_ref[...] + _block_mod(gate_ref, i * tm, tm, mixed) * y
    if final:
        out = out * lax.rsqrt(jnp.mean(out * out, axis=-1, keepdims=True) + EPS) * fg_ref[...]
    o_ref[...] = out


def _combine(ybuf, pos1, pos2, xa, route, mod, layer, final_g, final):
    m = xa.shape[0]
    tm = ROW_TILE
    kern = functools.partial(_combine_kernel, tm=tm, mixed=m > SEQ, final=final)
    grid_spec = pltpu.PrefetchScalarGridSpec(
        num_scalar_prefetch=2,
        grid=(m // tm,),
        in_specs=[
            pl.BlockSpec(memory_space=pl.ANY),
            pl.BlockSpec((tm, D), lambda i, p1, p2: (i, 0)),
            pl.BlockSpec((tm, LANES), lambda i, p1, p2: (i, 0)),
            pl.BlockSpec((1, SUBLANES, D), lambda i, p1, p2: (layer, 0, 5)),
            pl.BlockSpec((1, D), lambda i, p1, p2: (0, 0)),
        ],
        out_specs=pl.BlockSpec((tm, D), lambda i, p1, p2: (i, 0)),
        scratch_shapes=[pltpu.VMEM((tm, D), f32), pltpu.VMEM((tm, D), f32),
                        pltpu.SemaphoreType.DMA(())],
    )
    return pl.pallas_call(
        kern,
        out_shape=jax.ShapeDtypeStruct((m, D), f32),
        grid_spec=grid_spec,
        compiler_params=_params("arbitrary"),
        name="moe_combine",
    )(pos1, pos2, ybuf, xa, route, mod, final_g)


def _moe(xa, g, mod, layer, w_router, w1, w3, w2, j, final_g, final):
    m = xa.shape[0]
    wr = jnp.zeros((D, LANES), f32).at[:, :E].set(w_router)
    h, route, cnt = _router(xa, g, mod, layer, wr)

    e12 = route[:, 0:2].astype(jnp.int32)
    r12 = route[:, 4:6].astype(jnp.int32)
    counts = cnt[0, :E].astype(jnp.int32)
    nblocks = (counts + MOE_TILE - 1) // MOE_TILE
    bends = jnp.cumsum(nblocks)
    bstarts = bends - nblocks
    pos = bstarts[e12] * MOE_TILE + r12
    nblk = -(-(2 * m + E * (MOE_TILE - 1)) // MOE_TILE)
    blk = jnp.arange(nblk, dtype=jnp.int32)
    n_used = bends[-1]
    blk_c = jnp.minimum(blk, n_used - 1)
    block_expert = jnp.minimum(jnp.sum(bends[None, :] <= blk_c[:, None], axis=1), E - 1).astype(jnp.int32)
    in_expert = (blk_c - bstarts[block_expert]) * MOE_TILE
    nvalid = jnp.where(blk < n_used, jnp.minimum(counts[block_expert] - in_expert, MOE_TILE), 0)
    token = jnp.broadcast_to(jnp.arange(m, dtype=jnp.int32)[:, None], (m, 2))
    tok = jnp.zeros((nblk * MOE_TILE + GATHER_ROWS - MOE_TILE,), jnp.int32)
    tok = tok.at[pos.reshape(-1)].set(token.reshape(-1))

    ybuf = _experts(h, block_expert, nvalid.astype(jnp.int32), tok, w1, w3, w2, j, nblk)
    return _combine(ybuf, pos[:, 0], pos[:, 1], xa, route, mod, layer, final_g, final)


def kernel(x, c, ctx, c_ctx, norm_g, final_g, w_mod, b_mod, lru_w_in, lru_conv_w, lru_conv_b, lru_w_a, lru_b_a, lru_w_x, lru_b_x, lru_lambda, lru_w_out, sgu_w_in, sgu_ln_g, sgu_ln_b, sgu_w_s, sgu_b_s, sgu_w_out, ffn_w1, ffn_w3, ffn_w2, moe_router, moe_w1, moe_w3, moe_w2):
    assert x.shape == (1, SEQ, D) and ctx.shape == (1, CTX, D)
    xa = jnp.concatenate([x[0], ctx[0]], axis=0)
    cb = jnp.stack([jnp.broadcast_to(c[0][:, None], (D, LANES)),
                    jnp.broadcast_to(c_ctx[:, None], (D, LANES))])
    mod = _mod_table(cb, w_mod, b_mod)
    fg = final_g.reshape(1, D)

    for layer in range(DEPTH):
        j = layer // 2
        ctx_next = layer < 2
        g_mix = norm_g[layer, 0].reshape(1, D)
        g_ch = norm_g[layer, 1].reshape(1, D)
        if layer % 2 == 0:
            z = _inproj(xa, g_mix, mod, layer, lru_w_in, j, act=False)
            yf, yr = _lru_scan(z, j, lru_conv_w[j], lru_conv_b[j].reshape(1, W), lru_w_a, lru_w_x,
                               lru_b_a[j].reshape(2, W), lru_b_x[j].reshape(2, W), lru_lambda[j])
            m_out = SEQ + CTX if ctx_next else SEQ
            xa = _lru_out(yf, yr, z, xa, mod, layer, lru_w_out, j, m_out)
            xa = _ffn(xa, g_ch, mod, layer, ffn_w1, ffn_w3, ffn_w2, j)
        else:
            z = _inproj(xa, g_mix, mod, layer, sgu_w_in, j, act=True)
            xa = _sgu_out(z, j, sgu_ln_g[j].reshape(1, W), sgu_ln_b[j].reshape(1, W), sgu_w_s,
                          sgu_b_s[j].T, xa, mod, layer, sgu_w_out)
            xa = _moe(xa, g_ch, mod, layer, moe_router[j], moe_w1, moe_w3, moe_w2, j, fg,
                      final=layer == DEPTH - 1)
    return xa[None]
```
